```python
import jax, jax.numpy as jnp
from jax import lax
import numpy as np

D_MODEL = 2048
BATCH = 2
SEQ = 8192
DEPTH = 2

HEAD_DIM = 128
POOL_WINDOWS = (2, 4, 8, 16)
POOL_WIDTH = D_MODEL // 4
POOL_GROUP = POOL_WIDTH // len(POOL_WINDOWS)
SGU_WIDTH = (D_MODEL - POOL_WIDTH) // 2
SGU_HEADS = SGU_WIDTH // HEAD_DIM
CHUNK = 128
CONV_WIDTH = D_MODEL - POOL_WIDTH - SGU_WIDTH
CONV_GROUPS = CONV_WIDTH // HEAD_DIM
CONV_KERNEL = 31
IN_WIDTH = POOL_WIDTH + 2 * SGU_WIDTH + 2 * CONV_WIDTH
D_FF = 4 * D_MODEL
DEEPNORM_ALPHA = (2 * DEPTH) ** 0.25
DEEPNORM_BETA = (8 * DEPTH) ** -0.25
LN_EPS = 1e-5

kernel_name = "hybrid_pool_sgu_conv_deepnorm"


def layer_norm(x, g, b):
    xf = x.astype(jnp.float32)
    mu = jnp.mean(xf, axis=-1, keepdims=True)
    xc = xf - mu
    var = jnp.mean(jnp.square(xc), axis=-1, keepdims=True)
    y = xc * lax.rsqrt(var + LN_EPS)
    return (y * g.astype(jnp.float32) + b.astype(jnp.float32)).astype(x.dtype)


def pool_mixer(a, w_pool, pool_scale):
    bsz, s, _ = a.shape
    cs = jnp.cumsum(a.astype(jnp.float32), axis=1)
    count = jnp.arange(1, s + 1, dtype=jnp.float32)[None, :, None]
    means = []
    for g, win in enumerate(POOL_WINDOWS):
        c = cs[..., g * POOL_GROUP:(g + 1) * POOL_GROUP]
        prev = jnp.pad(c[:, :-win], ((0, 0), (win, 0), (0, 0)))
        means.append((c - prev) / jnp.minimum(count, float(win)))
    pooled = jnp.concatenate(means, axis=-1).astype(a.dtype) - a
    pooled = pooled.reshape(bsz, s, len(POOL_WINDOWS), POOL_GROUP)
    y = jnp.einsum('bsgc,gcd->bsgd', pooled, w_pool).reshape(bsz, s, POOL_WIDTH)
    return y * pool_scale


def sgu_mixer(uv, ln_g, ln_b, w_s, b_s):
    bsz, s, _ = uv.shape
    uv = jax.nn.gelu(uv)
    u, v = jnp.split(uv, 2, axis=-1)
    v = layer_norm(v, ln_g, ln_b)
    vc = v.reshape(bsz, s // CHUNK, CHUNK, SGU_HEADS, HEAD_DIM)
    mask = jnp.tril(jnp.ones((CHUNK, CHUNK), dtype=w_s.dtype))
    mixed = jnp.einsum('hts,bnshc->bnthc', w_s * mask, vc) + b_s.T[None, None, :, :, None]
    return u * mixed.reshape(bsz, s, SGU_WIDTH)


def conv_module(ag, conv_w, conv_b, ln_g, ln_b):
    a, g = jnp.split(ag, 2, axis=-1)
    h = a * jax.nn.sigmoid(g)
    h = lax.conv_general_dilated(
        h, conv_w[:, None, :], window_strides=(1,), padding=[(CONV_KERNEL - 1, 0)],
        dimension_numbers=('NWC', 'WIO', 'NWC'), feature_group_count=CONV_WIDTH) + conv_b
    h = layer_norm(h, ln_g, ln_b)
    return jax.nn.silu(h)


def setup_inputs(seed: int = 0) -> dict:
    key = jax.random.key(seed)
    ks = jax.random.split(key, 24)

    def nrm(k, shape, scale):
        return jax.random.normal(k, shape, dtype=jnp.float32) * scale

    L = DEPTH
    return {
        "x": nrm(ks[0], (BATCH, SEQ, D_MODEL), 1.0),
        "w_in": nrm(ks[1], (L, D_MODEL, IN_WIDTH), D_MODEL ** -0.5),
        "b_in": nrm(ks[2], (L, IN_WIDTH), 0.02),
        "w_pool": nrm(ks[3], (L, len(POOL_WINDOWS), POOL_GROUP, POOL_GROUP), POOL_GROUP ** -0.5),
        "pool_scale": 1.0 + nrm(ks[4], (L, POOL_WIDTH), 0.1),
        "sgu_ln_g": 1.0 + nrm(ks[5], (L, SGU_WIDTH), 0.02),
        "sgu_ln_b": nrm(ks[6], (L, SGU_WIDTH), 0.02),
        "sgu_w": nrm(ks[7], (L, SGU_HEADS, CHUNK, CHUNK), CHUNK ** -0.5),
        "sgu_b": 1.0 + nrm(ks[8], (L, SGU_HEADS, CHUNK), 0.02),
        "conv_w": nrm(ks[9], (L, CONV_KERNEL, CONV_WIDTH), CONV_KERNEL ** -0.5),
        "conv_b": nrm(ks[10], (L, CONV_WIDTH), 0.02),
        "conv_ln_g": 1.0 + nrm(ks[11], (L, CONV_WIDTH), 0.02),
        "conv_ln_b": nrm(ks[12], (L, CONV_WIDTH), 0.02),
        "w_out": nrm(ks[13], (L, D_MODEL, D_MODEL), DEEPNORM_BETA * D_MODEL ** -0.5),
        "b_out": nrm(ks[14], (L, D_MODEL), 0.02),
        "ln1_g": 1.0 + nrm(ks[15], (L, D_MODEL), 0.02),
        "ln1_b": nrm(ks[16], (L, D_MODEL), 0.02),
        "w_ff1": nrm(ks[17], (L, D_MODEL, D_FF), D_MODEL ** -0.5),
        "b_ff1": nrm(ks[18], (L, D_FF), 0.02),
        "w_ff2": nrm(ks[19], (L, D_FF, D_MODEL), DEEPNORM_BETA * D_FF ** -0.5),
        "b_ff2": nrm(ks[20], (L, D_MODEL), 0.02),
        "ln2_g": 1.0 + nrm(ks[21], (L, D_MODEL), 0.02),
        "ln2_b": nrm(ks[22], (L, D_MODEL), 0.02),
    }


def reference(x, w_in, b_in, w_pool, pool_scale, sgu_ln_g, sgu_ln_b, sgu_w, sgu_b,
              conv_w, conv_b, conv_ln_g, conv_ln_b, w_out, b_out, ln1_g, ln1_b,
              w_ff1, b_ff1, w_ff2, b_ff2, ln2_g, ln2_b):
    for l in range(DEPTH):
        proj = jnp.einsum('bsd,de->bse', x, w_in[l]) + b_in[l]
        p_a = proj[..., :POOL_WIDTH]
        p_b = proj[..., POOL_WIDTH:POOL_WIDTH + 2 * SGU_WIDTH]
        p_c = proj[..., POOL_WIDTH + 2 * SGU_WIDTH:]
        y_a = pool_mixer(p_a, w_pool[l], pool_scale[l])
        y_b = sgu_mixer(p_b, sgu_ln_g[l], sgu_ln_b[l], sgu_w[l], sgu_b[l])
        y_c = conv_module(p_c, conv_w[l], conv_b[l], conv_ln_g[l], conv_ln_b[l])
        mixed = jnp.concatenate([y_a, y_b, y_c], axis=-1)
        mix_out = jnp.einsum('bsd,de->bse', mixed, w_out[l]) + b_out[l]
        x = layer_norm(DEEPNORM_ALPHA * x + mix_out, ln1_g[l], ln1_b[l])
        h = jnp.square(jax.nn.relu(jnp.einsum('bsd,df->bsf', x, w_ff1[l]) + b_ff1[l]))
        ff_out = jnp.einsum('bsf,fd->bsd', h, w_ff2[l]) + b_ff2[l]
        x = layer_norm(DEEPNORM_ALPHA * x + ff_out, ln2_g[l], ln2_b[l])
    return x
```

```python
import functools

import jax
import jax.numpy as jnp
from jax import lax
from jax.experimental import pallas as pl
from jax.experimental.pallas import tpu as pltpu

D_MODEL = 2048
DEPTH = 2
HEAD_DIM = 128
POOL_WINDOWS = (2, 4, 8, 16)
POOL_WIDTH = D_MODEL // 4
POOL_GROUP = POOL_WIDTH // len(POOL_WINDOWS)
SGU_WIDTH = (D_MODEL - POOL_WIDTH) // 2
SGU_HEADS = SGU_WIDTH // HEAD_DIM
CHUNK = 128
CONV_WIDTH = D_MODEL - POOL_WIDTH - SGU_WIDTH
CONV_KERNEL = 31
IN_WIDTH = POOL_WIDTH + 2 * SGU_WIDTH + 2 * CONV_WIDTH
D_FF = 4 * D_MODEL
DEEPNORM_ALPHA = (2 * DEPTH) ** 0.25
LN_EPS = 1e-5

SGU_U_COL = POOL_WIDTH
SGU_V_COL = POOL_WIDTH + SGU_WIDTH
CONV_A_COL = POOL_WIDTH + 2 * SGU_WIDTH
CONV_G_COL = CONV_A_COL + CONV_WIDTH
SGU_OUT_COL = POOL_WIDTH
CONV_OUT_COL = POOL_WIDTH + SGU_WIDTH

LANES = 128
SUBLANES = 8
VMEM_LIMIT_BYTES = 56 * 1024 * 1024

ROW_TILE = 512
SEQ_TILE = 256
POOL_HALO = 16
CONV_HALO = 32
CONV_ROWS = 64
FFN_ROW_TILE = 1024
FFN_COL_TILE = 512

BF16 = jnp.bfloat16
F32 = jnp.float32


def _layer_norm(x, g, b):
    mu = jnp.mean(x, axis=-1, keepdims=True)
    xc = x - mu
    var = jnp.mean(xc * xc, axis=-1, keepdims=True)
    return xc * lax.rsqrt(var + LN_EPS) * g + b


def _resident(shape):
    return pl.BlockSpec(shape, lambda *_: (0,) * len(shape), pipeline_mode=pl.Buffered(1))


def _inproj_kernel(x_ref, w_ref, b_ref, o_ref):
    xb = x_ref[...].astype(BF16)
    o_ref[...] = jnp.dot(xb, w_ref[...], preferred_element_type=F32) + b_ref[...]


def _inproj(x, w, b):
    m = x.shape[0]
    return pl.pallas_call(
        _inproj_kernel,
        grid=(m // ROW_TILE,),
        in_specs=[
            pl.BlockSpec((ROW_TILE, D_MODEL), lambda i: (i, 0)),
            _resident((D_MODEL, IN_WIDTH)),
            _resident((1, IN_WIDTH)),
        ],
        out_specs=pl.BlockSpec((ROW_TILE, IN_WIDTH), lambda i: (i, 0)),
        out_shape=jax.ShapeDtypeStruct((m, IN_WIDTH), F32),
        compiler_params=pltpu.CompilerParams(
            dimension_semantics=("arbitrary",), vmem_limit_bytes=VMEM_LIMIT_BYTES),
        name="inproj",
    )(x, w, b)


def _mixer_kernel(proj_ref, wpool_ref, pscale_ref, sgu_g_ref, sgu_b_ref, ws_ref, bs_ref,
                  cw_ref, cb_ref, cg_ref, cbeta_ref, out_ref, pool_halo, hbuf, cbuf):
    ts = SEQ_TILE
    i = pl.program_id(1)

    @pl.when(i == 0)
    def _():
        pool_halo[...] = jnp.zeros_like(pool_halo)
        hbuf[0:CONV_HALO, :] = jnp.zeros((CONV_HALO, CONV_WIDTH), F32)

    row = i * ts + lax.broadcasted_iota(jnp.int32, (ts, LANES), 0)
    for g, win in enumerate(POOL_WINDOWS):
        cs = slice(g * POOL_GROUP, (g + 1) * POOL_GROUP)
        cur = proj_ref[:, cs]
        s = jnp.concatenate([pool_halo[:, cs], cur], axis=0)
        shift = 1
        while shift < win:
            s = s + pltpu.roll(s, shift, 0)
            shift *= 2
        wsum = s[POOL_HALO:, :]
        cnt = jnp.minimum(row + 1, win).astype(F32)
        pooled = wsum / cnt - cur
        y = jnp.dot(pooled.astype(BF16), wpool_ref[g], preferred_element_type=F32)
        out_ref[:, cs] = (y * pscale_ref[:, cs]).astype(BF16)
        pool_halo[:, cs] = cur[ts - POOL_HALO:, :]

    tri_r = lax.broadcasted_iota(jnp.int32, (CHUNK, CHUNK), 0)
    tri_c = lax.broadcasted_iota(jnp.int32, (CHUNK, CHUNK), 1)
    ws = [jnp.where(tri_r >= tri_c, ws_ref[h], 0.0).astype(BF16) for h in range(SGU_HEADS)]
    for n in range(ts // CHUNK):
        rs = slice(n * CHUNK, (n + 1) * CHUNK)
        u = jax.nn.gelu(proj_ref[rs, SGU_U_COL:SGU_U_COL + SGU_WIDTH])
        v = jax.nn.gelu(proj_ref[rs, SGU_V_COL:SGU_V_COL + SGU_WIDTH])
        vb = _layer_norm(v, sgu_g_ref[...], sgu_b_ref[...]).astype(BF16)
        for h in range(SGU_HEADS):
            hs = slice(h * HEAD_DIM, (h + 1) * HEAD_DIM)
            m = jnp.dot(ws[h], vb[:, hs], preferred_element_type=F32) + bs_ref[:, hs]
            out_ref[rs, SGU_OUT_COL + h * HEAD_DIM:SGU_OUT_COL + (h + 1) * HEAD_DIM] = (
                u[:, hs] * m).astype(BF16)

    a = proj_ref[:, CONV_A_COL:CONV_A_COL + CONV_WIDTH]
    gate = proj_ref[:, CONV_G_COL:CONV_G_COL + CONV_WIDTH]
    hbuf[CONV_HALO:CONV_HALO + ts, :] = a * jax.nn.sigmoid(gate)
    tap0 = CONV_HALO - (CONV_KERNEL - 1)
    for c in range(CONV_WIDTH // LANES):
        cs = slice(c * LANES, (c + 1) * LANES)
        wcol = cw_ref[:, cs]
        bias = cb_ref[:, cs]
        for rb in range(ts // CONV_ROWS):
            r0 = rb * CONV_ROWS
            acc = jnp.broadcast_to(bias, (CONV_ROWS, LANES))
            for k in range(CONV_KERNEL):
                acc = acc + wcol[k:k + 1, :] * hbuf[r0 + tap0 + k:r0 + tap0 + k + CONV_ROWS, cs]
            cbuf[r0:r0 + CONV_ROWS, cs] = acc
    hbuf[0:CONV_HALO, :] = hbuf[ts:ts + CONV_HALO, :]
    for n in range(ts // CHUNK):
        rs = slice(n * CHUNK, (n + 1) * CHUNK)
        y = _layer_norm(cbuf[rs, :], cg_ref[...], cbeta_ref[...])
        out_ref[rs, CONV_OUT_COL:CONV_OUT_COL + CONV_WIDTH] = jax.nn.silu(y).astype(BF16)


def _mixer(proj, wpool, pscale, sgu_g, sgu_b, ws, bs_full, cw, cb, cg, cbeta):
    bsz, s, _ = proj.shape
    return pl.pallas_call(
        _mixer_kernel,
        grid=(bsz, s // SEQ_TILE),
        in_specs=[
            pl.BlockSpec((None, SEQ_TILE, IN_WIDTH), lambda b, i: (b, i, 0)),
            _resident((len(POOL_WINDOWS), POOL_GROUP, POOL_GROUP)),
            _resident((1, POOL_WIDTH)),
            _resident((1, SGU_WIDTH)),
            _resident((1, SGU_WIDTH)),
            _resident((SGU_HEADS, CHUNK, CHUNK)),
            _resident((CHUNK, SGU_WIDTH)),
            _resident((CONV_KERNEL, CONV_WIDTH)),
            _resident((1, CONV_WIDTH)),
            _resident((1, CONV_WIDTH)),
            _resident((1, CONV_WIDTH)),
        ],
        out_specs=pl.BlockSpec((None, SEQ_TILE, D_MODEL), lambda b, i: (b, i, 0)),
        out_shape=jax.ShapeDtypeStruct((bsz, s, D_MODEL), BF16),
        scratch_shapes=[
            pltpu.VMEM((POOL_HALO, POOL_WIDTH), F32),
            pltpu.VMEM((CONV_HALO + SEQ_TILE, CONV_WIDTH), F32),
            pltpu.VMEM((SEQ_TILE, CONV_WIDTH), F32),
        ],
        compiler_params=pltpu.CompilerParams(
            dimension_semantics=("arbitrary", "arbitrary"), vmem_limit_bytes=VMEM_LIMIT_BYTES),
        name="mixer",
    )(proj, wpool, pscale, sgu_g, sgu_b, ws, bs_full, cw, cb, cg, cbeta)


def _outproj_kernel(m_ref, x_ref, w_ref, b_ref, g_ref, beta_ref, o_ref):
    y = jnp.dot(m_ref[...], w_ref[...], preferred_element_type=F32) + b_ref[...]
    o_ref[...] = _layer_norm(DEEPNORM_ALPHA * x_ref[...] + y, g_ref[...], beta_ref[...])


def _outproj(mixed, x, w, b, g, beta):
    m = x.shape[0]
    return pl.pallas_call(
        _outproj_kernel,
        grid=(m // ROW_TILE,),
        in_specs=[
            pl.BlockSpec((ROW_TILE, D_MODEL), lambda i: (i, 0)),
            pl.BlockSpec((ROW_TILE, D_MODEL), lambda i: (i, 0)),
            _resident((D_MODEL, D_MODEL)),
            _resident((1, D_MODEL)),
            _resident((1, D_MODEL)),
            _resident((1, D_MODEL)),
        ],
        out_specs=pl.BlockSpec((ROW_TILE, D_MODEL), lambda i: (i, 0)),
        out_shape=jax.ShapeDtypeStruct((m, D_MODEL), F32),
        compiler_params=pltpu.CompilerParams(
            dimension_semantics=("arbitrary",), vmem_limit_bytes=VMEM_LIMIT_BYTES),
        name="outproj",
    )(mixed, x, w, b, g, beta)


def _ffn_kernel(x_ref, w1_ref, b1_ref, w2_ref, b2_ref, g_ref, beta_ref, o_ref, xb_ref):
    k = pl.program_id(1)

    @pl.when(k == 0)
    def _():
        xb_ref[...] = x_ref[...].astype(BF16)
        o_ref[...] = jnp.zeros_like(o_ref)

    h = jnp.dot(xb_ref[...], w1_ref[...], preferred_element_type=F32) + b1_ref[...]
    h = jnp.square(jnp.maximum(h, 0.0)).astype(BF16)
    o_ref[...] += jnp.dot(h, w2_ref[...], preferred_element_type=F32)

    @pl.when(k == pl.num_programs(1) - 1)
    def _():
        z = DEEPNORM_ALPHA * x_ref[...] + (o_ref[...] + b2_ref[...])
        o_ref[...] = _layer_norm(z, g_ref[...], beta_ref[...])


def _ffn(x, w1, b1, w2, b2, g, beta):
    m = x.shape[0]
    return pl.pallas_call(
        _ffn_kernel,
        grid=(m // FFN_ROW_TILE, D_FF // FFN_COL_TILE),
        in_specs=[
            pl.BlockSpec((FFN_ROW_TILE, D_MODEL), lambda i, k: (i, 0)),
            pl.BlockSpec((D_MODEL, FFN_COL_TILE), lambda i, k: (0, k)),
            pl.BlockSpec((1, FFN_COL_TILE), lambda i, k: (0, k)),
            pl.BlockSpec((FFN_COL_TILE, D_MODEL), lambda i, k: (k, 0)),
            _resident((1, D_MODEL)),
            _resident((1, D_MODEL)),
            _resident((1, D_MODEL)),
        ],
        out_specs=pl.BlockSpec((FFN_ROW_TILE, D_MODEL), lambda i, k: (i, 0)),
        out_shape=jax.ShapeDtypeStruct((m, D_MODEL), F32),
        scratch_shapes=[pltpu.VMEM((FFN_ROW_TILE, D_MODEL), BF16)],
        compiler_params=pltpu.CompilerParams(
            dimension_semantics=("arbitrary", "arbitrary"), vmem_limit_bytes=VMEM_LIMIT_BYTES),
        name="ffn",
    )(x, w1, b1, w2, b2, g, beta)


def kernel(x, w_in, b_in, w_pool, pool_scale, sgu_ln_g, sgu_ln_b, sgu_w, sgu_b, conv_w, conv_b,
           conv_ln_g, conv_ln_b, w_out, b_out, ln1_g, ln1_b, w_ff1, b_ff1, w_ff2, b_ff2, ln2_g, ln2_b):
    bsz, s, d = x.shape
    assert d == D_MODEL and s % SEQ_TILE == 0 and (bsz * s) % FFN_ROW_TILE == 0
    xf = x.reshape(bsz * s, d)
    row = lambda p: p.reshape(1, -1)
    for l in range(DEPTH):
        proj = _inproj(xf, w_in[l].astype(BF16), row(b_in[l]))
        bs_full = jnp.repeat(sgu_b[l].T, HEAD_DIM, axis=1)
        mixed = _mixer(proj.reshape(bsz, s, IN_WIDTH), w_pool[l].astype(BF16), row(pool_scale[l]),
                       row(sgu_ln_g[l]), row(sgu_ln_b[l]), sgu_w[l], bs_full,
                       conv_w[l], row(conv_b[l]), row(conv_ln_g[l]), row(conv_ln_b[l]))
        x1 = _outproj(mixed.reshape(bsz * s, d), xf, w_out[l].astype(BF16), row(b_out[l]),
                      row(ln1_g[l]), row(ln1_b[l]))
        xf = _ffn(x1, w_ff1[l].astype(BF16), row(b_ff1[l]), w_ff2[l].astype(BF16), row(b_ff2[l]),
                  row(ln2_g[l]), row(ln2_b[l]))
    return xf.reshape(bsz, s, d)
```

```python
import jax
import jax.numpy as jnp
from jax import lax
from jax.experimental import pallas as pl
from jax.experimental.pallas import tpu as pltpu

D_MODEL = 2048
DEPTH = 2
HEAD_DIM = 128
POOL_WINDOWS = (2, 4, 8, 16)
POOL_WIDTH = D_MODEL // 4
POOL_GROUP = POOL_WIDTH // len(POOL_WINDOWS)
SGU_WIDTH = (D_MODEL - POOL_WIDTH) // 2
SGU_HEADS = SGU_WIDTH // HEAD_DIM
CHUNK = 128
CONV_WIDTH = D_MODEL - POOL_WIDTH - SGU_WIDTH
CONV_KERNEL = 31
IN_WIDTH = POOL_WIDTH + 2 * SGU_WIDTH + 2 * CONV_WIDTH
D_FF = 4 * D_MODEL
DEEPNORM_ALPHA = (2 * DEPTH) ** 0.25
LN_EPS = 1e-5

SGU_U_COL = POOL_WIDTH
SGU_V_COL = POOL_WIDTH + SGU_WIDTH
CONV_A_COL = POOL_WIDTH + 2 * SGU_WIDTH
CONV_G_COL = CONV_A_COL + CONV_WIDTH
SGU_OUT_COL = POOL_WIDTH
CONV_OUT_COL = POOL_WIDTH + SGU_WIDTH

LANES = 128
SUBLANES = 8
BF16_ROWS = 16
VMEM_LIMIT_BYTES = 56 * 1024 * 1024

SEQ_TILE = 256
POOL_HALO = 16
CONV_HALO = 32
CONV_ROWS = 64
ROW_PITCH = 2
FFN_ROW_TILE = 1024
FFN_COL_TILE = 512

BF16 = jnp.bfloat16
F32 = jnp.float32


def _layer_norm(x, g, b):
    mu = jnp.mean(x, axis=-1, keepdims=True)
    xc = x - mu
    var = jnp.mean(xc * xc, axis=-1, keepdims=True)
    return xc * lax.rsqrt(var + LN_EPS) * g + b


def _resident(shape):
    return pl.BlockSpec(shape, lambda *_: (0,) * len(shape), pipeline_mode=pl.Buffered(1))


def _rows(start, size):
    return pl.ds(ROW_PITCH * start, size, stride=ROW_PITCH)


def _mix_kernel(x_ref, w_in_ref, b_in_ref, wpool_ref, pscale_ref, sgu_g_ref, sgu_b_ref, ws_ref, bs_ref,
                cw_ref, cb_ref, cg_ref, cbeta_ref, w_out_ref, b_out_ref, g1_ref, beta1_ref,
                wf1_ref, wf2_ref,
                o_ref, wf1b_ref, wf2b_ref,
                proj, mixed, pbuf, hbuf, cbuf):
    ts = SEQ_TILE
    i = pl.program_id(1)

    wf1b_ref[...] = wf1_ref[...].astype(BF16)
    wf2b_ref[...] = wf2_ref[...].astype(BF16)

    @pl.when(i == 0)
    def _():
        for g in range(len(POOL_WINDOWS)):
            pbuf[g, _rows(0, POOL_HALO), :] = jnp.zeros((POOL_HALO, LANES), F32)
        for c in range(CONV_WIDTH // LANES):
            hbuf[c, _rows(0, CONV_HALO), :] = jnp.zeros((CONV_HALO, LANES), F32)

    xb = x_ref[...].astype(BF16)
    for lo, hi in ((CONV_A_COL, IN_WIDTH), (SGU_U_COL, CONV_A_COL), (0, POOL_WIDTH)):
        proj[:, lo:hi] = (jnp.dot(xb, w_in_ref[:, lo:hi], preferred_element_type=F32)
                          + b_in_ref[:, lo:hi])

    a = proj[:, CONV_A_COL:CONV_A_COL + CONV_WIDTH]
    gate = proj[:, CONV_G_COL:CONV_G_COL + CONV_WIDTH]
    h = a * jax.nn.sigmoid(gate)
    for c in range(CONV_WIDTH // LANES):
        hbuf[c, _rows(CONV_HALO, ts), :] = h[:, c * LANES:(c + 1) * LANES]
    tap0 = CONV_HALO - (CONV_KERNEL - 1)
    for c in range(CONV_WIDTH // LANES):
        cs = slice(c * LANES, (c + 1) * LANES)
        for rb in range(ts // CONV_ROWS):
            r0 = rb * CONV_ROWS
            acc = jnp.broadcast_to(cb_ref[:, cs], (CONV_ROWS, LANES))
            for k in range(CONV_KERNEL):
                acc = acc + cw_ref[k:k + 1, cs] * hbuf[c, _rows(r0 + tap0 + k, CONV_ROWS), :]
            cbuf[r0:r0 + CONV_ROWS, cs] = acc
        hbuf[c, _rows(0, CONV_HALO), :] = hbuf[c, _rows(ts, CONV_HALO), :]
    for n in range(ts // CHUNK):
        rs = slice(n * CHUNK, (n + 1) * CHUNK)
        y = _layer_norm(cbuf[rs, :], cg_ref[...], cbeta_ref[...])
        mixed[rs, CONV_OUT_COL:CONV_OUT_COL + CONV_WIDTH] = jax.nn.silu(y).astype(BF16)

    tri_r = lax.broadcasted_iota(jnp.int32, (CHUNK, CHUNK), 0)
    tri_c = lax.broadcasted_iota(jnp.int32, (CHUNK, CHUNK), 1)
    ws = [jnp.where(tri_r >= tri_c, ws_ref[hd], 0.0).astype(BF16) for hd in range(SGU_HEADS)]
    for n in range(ts // CHUNK):
        rs = slice(n * CHUNK, (n + 1) * CHUNK)
        u = jax.nn.gelu(proj[rs, SGU_U_COL:SGU_U_COL + SGU_WIDTH])
        v = jax.nn.gelu(proj[rs, SGU_V_COL:SGU_V_COL + SGU_WIDTH])
        vb = _layer_norm(v, sgu_g_ref[...], sgu_b_ref[...]).astype(BF16)
        for hd in range(SGU_HEADS):
            hs = slice(hd * HEAD_DIM, (hd + 1) * HEAD_DIM)
            m = jnp.dot(ws[hd], vb[:, hs], preferred_element_type=F32) + bs_ref[:, hs]
            mixed[rs, SGU_OUT_COL + hd * HEAD_DIM:SGU_OUT_COL + (hd + 1) * HEAD_DIM] = (
                u[:, hs] * m).astype(BF16)

    row = i * ts + lax.broadcasted_iota(jnp.int32, (ts, LANES), 0)
    for g, win in enumerate(POOL_WINDOWS):
        cs = slice(g * POOL_GROUP, (g + 1) * POOL_GROUP)
        cur = proj[:, cs]
        pbuf[g, _rows(POOL_HALO, ts), :] = cur
        wsum = cur
        for k in range(1, win):
            wsum = wsum + pbuf[g, _rows(POOL_HALO - k, ts), :]
        pbuf[g, _rows(0, POOL_HALO), :] = pbuf[g, _rows(ts, POOL_HALO), :]
        cnt = jnp.minimum(row + 1, win).astype(F32)
        pooled = wsum / cnt - cur
        y = jnp.dot(pooled.astype(BF16), wpool_ref[g], preferred_element_type=F32)
        mixed[:, cs] = (y * pscale_ref[:, cs]).astype(BF16)

    y = jnp.dot(mixed[...], w_out_ref[...], preferred_element_type=F32) + b_out_ref[...]
    o_ref[...] = _layer_norm(DEEPNORM_ALPHA * x_ref[...] + y, g1_ref[...], beta1_ref[...])


def _mix(x, w_in, b_in, wpool, pscale, sgu_g, sgu_b, ws, bs_full, cw, cb, cg, cbeta,
         w_out, b_out, g1, beta1, w_ff1, w_ff2):
    bsz, s, _ = x.shape
    nt = s // SEQ_TILE
    steps = bsz * nt
    f1_rows = D_MODEL // steps
    f2_rows = D_FF // steps
    assert f1_rows * steps == D_MODEL and f1_rows % BF16_ROWS == 0
    step = lambda b, i: b * nt + i
    return pl.pallas_call(
        _mix_kernel,
        grid=(bsz, nt),
        in_specs=[
            pl.BlockSpec((None, SEQ_TILE, D_MODEL), lambda b, i: (b, i, 0)),
            _resident((D_MODEL, IN_WIDTH)),
            _resident((1, IN_WIDTH)),
            _resident((len(POOL_WINDOWS), POOL_GROUP, POOL_GROUP)),
            _resident((1, POOL_WIDTH)),
            _resident((1, SGU_WIDTH)),
            _resident((1, SGU_WIDTH)),
            _resident((SGU_HEADS, CHUNK, CHUNK)),
            _resident((CHUNK, SGU_WIDTH)),
            _resident((CONV_KERNEL, CONV_WIDTH)),
            _resident((1, CONV_WIDTH)),
            _resident((1, CONV_WIDTH)),
            _resident((1, CONV_WIDTH)),
            _resident((D_MODEL, D_MODEL)),
            _resident((1, D_MODEL)),
            _resident((1, D_MODEL)),
            _resident((1, D_MODEL)),
            pl.BlockSpec((f1_rows, D_FF), lambda b, i: (step(b, i), 0)),
            pl.BlockSpec((f2_rows, D_MODEL), lambda b, i: (step(b, i), 0)),
        ],
        out_specs=[
            pl.BlockSpec((None, SEQ_TILE, D_MODEL), lambda b, i: (b, i, 0)),
            pl.BlockSpec((f1_rows, D_FF), lambda b, i: (step(b, i), 0)),
            pl.BlockSpec((f2_rows, D_MODEL), lambda b, i: (step(b, i), 0)),
        ],
        out_shape=[
            jax.ShapeDtypeStruct((bsz, s, D_MODEL), F32),
            jax.ShapeDtypeStruct((D_MODEL, D_FF), BF16),
            jax.ShapeDtypeStruct((D_FF, D_MODEL), BF16),
        ],
        scratch_shapes=[
            pltpu.VMEM((SEQ_TILE, IN_WIDTH), F32),
            pltpu.VMEM((SEQ_TILE, D_MODEL), BF16),
            pltpu.VMEM((len(POOL_WINDOWS), ROW_PITCH * (POOL_HALO + SEQ_TILE), LANES), F32),
            pltpu.VMEM((CONV_WIDTH // LANES, ROW_PITCH * (CONV_HALO + SEQ_TILE), LANES), F32),
            pltpu.VMEM((SEQ_TILE, CONV_WIDTH), F32),
        ],
        compiler_params=pltpu.CompilerParams(
            dimension_semantics=("arbitrary", "arbitrary"), vmem_limit_bytes=VMEM_LIMIT_BYTES),
        name="mix",
    )(x, w_in, b_in, wpool, pscale, sgu_g, sgu_b, ws, bs_full, cw, cb, cg, cbeta,
      w_out, b_out, g1, beta1, w_ff1, w_ff2)


def _ffn_kernel(x_ref, w1_ref, b1_ref, w2_ref, b2_ref, g_ref, beta_ref, o_ref, xb_ref):
    k = pl.program_id(1)

    @pl.when(k == 0)
    def _():
        xb_ref[...] = x_ref[...].astype(BF16)
        o_ref[...] = jnp.zeros_like(o_ref)

    h = jnp.dot(xb_ref[...], w1_ref[...], preferred_element_type=F32) + b1_ref[...]
    h = jnp.square(jnp.maximum(h, 0.0)).astype(BF16)
    o_ref[...] += jnp.dot(h, w2_ref[...], preferred_element_type=F32)

    @pl.when(k == pl.num_programs(1) - 1)
    def _():
        z = DEEPNORM_ALPHA * x_ref[...] + (o_ref[...] + b2_ref[...])
        o_ref[...] = _layer_norm(z, g_ref[...], beta_ref[...])


def _ffn(x, w1, b1, w2, b2, g, beta):
    m = x.shape[0]
    return pl.pallas_call(
        _ffn_kernel,
        grid=(m // FFN_ROW_TILE, D_FF // FFN_COL_TILE),
        in_specs=[
            pl.BlockSpec((FFN_ROW_TILE, D_MODEL), lambda i, k: (i, 0)),
            pl.BlockSpec((D_MODEL, FFN_COL_TILE), lambda i, k: (0, k)),
            pl.BlockSpec((1, FFN_COL_TILE), lambda i, k: (0, k)),
            pl.BlockSpec((FFN_COL_TILE, D_MODEL), lambda i, k: (k, 0)),
            _resident((1, D_MODEL)),
            _resident((1, D_MODEL)),
            _resident((1, D_MODEL)),
        ],
        out_specs=pl.BlockSpec((FFN_ROW_TILE, D_MODEL), lambda i, k: (i, 0)),
        out_shape=jax.ShapeDtypeStruct((m, D_MODEL), F32),
        scratch_shapes=[pltpu.VMEM((FFN_ROW_TILE, D_MODEL), BF16)],
        compiler_params=pltpu.CompilerParams(
            dimension_semantics=("arbitrary", "arbitrary"), vmem_limit_bytes=VMEM_LIMIT_BYTES),
        name="ffn",
    )(x, w1, b1, w2, b2, g, beta)


def kernel(x, w_in, b_in, w_pool, pool_scale, sgu_ln_g, sgu_ln_b, sgu_w, sgu_b, conv_w, conv_b,
           conv_ln_g, conv_ln_b, w_out, b_out, ln1_g, ln1_b, w_ff1, b_ff1, w_ff2, b_ff2, ln2_g, ln2_b):
    bsz, s, d = x.shape
    assert d == D_MODEL and s % SEQ_TILE == 0 and (bsz * s) % FFN_ROW_TILE == 0
    row = lambda p: p.reshape(1, -1)
    for l in range(DEPTH):
        bs_full = jnp.repeat(sgu_b[l].T, HEAD_DIM, axis=1)
        x1, w1b, w2b = _mix(
            x, w_in[l].astype(BF16), row(b_in[l]), w_pool[l].astype(BF16), row(pool_scale[l]),
            row(sgu_ln_g[l]), row(sgu_ln_b[l]), sgu_w[l], bs_full,
            conv_w[l], row(conv_b[l]), row(conv_ln_g[l]), row(conv_ln_b[l]),
            w_out[l].astype(BF16), row(b_out[l]), row(ln1_g[l]), row(ln1_b[l]), w_ff1[l], w_ff2[l])
        x = _ffn(x1.reshape(bsz * s, d), w1b, row(b_ff1[l]), w2b, row(b_ff2[l]),
                 row(ln2_g[l]), row(ln2_b[l])).reshape(bsz, s, d)
    return x
```

```python
import jax
import jax.numpy as jnp
from jax import lax
from jax.experimental import pallas as pl
from jax.experimental.pallas import tpu as pltpu

D_MODEL = 2048
DEPTH = 2
HEAD_DIM = 128
POOL_WINDOWS = (2, 4, 8, 16)
POOL_WIDTH = D_MODEL // 4
POOL_GROUP = POOL_WIDTH // len(POOL_WINDOWS)
SGU_WIDTH = (D_MODEL - POOL_WIDTH) // 2
SGU_HEADS = SGU_WIDTH // HEAD_DIM
CHUNK = 128
CONV_WIDTH = D_MODEL - POOL_WIDTH - SGU_WIDTH
CONV_KERNEL = 31
IN_WIDTH = POOL_WIDTH + 2 * SGU_WIDTH + 2 * CONV_WIDTH
D_FF = 4 * D_MODEL
DEEPNORM_ALPHA = (2 * DEPTH) ** 0.25
LN_EPS = 1e-5

SGU_U_COL = POOL_WIDTH
SGU_V_COL = POOL_WIDTH + SGU_WIDTH
CONV_A_COL = POOL_WIDTH + 2 * SGU_WIDTH
SGU_OUT_COL = POOL_WIDTH
CONV_OUT_COL = POOL_WIDTH + SGU_WIDTH

LANES = 128
SUBLANES = 8
BF16_ROWS = 16
VMEM_LIMIT_BYTES = 56 * 1024 * 1024

SEQ_TILE = 256
POOL_HALO = 16
CONV_HALO = 32
CONV_ROWS = 64
ROW_BLOCK = 32
LN_ROWS = 16
ROW_PITCH = 2
FFN_ROW_TILE = 1024
FFN_COL_TILE = 512

BF16 = jnp.bfloat16
F32 = jnp.float32


def _layer_norm(x, g, b):
    mu = jnp.mean(x, axis=-1, keepdims=True)
    xc = x - mu
    var = jnp.mean(xc * xc, axis=-1, keepdims=True)
    return xc * lax.rsqrt(var + LN_EPS) * g + b


def _resident(shape):
    return pl.BlockSpec(shape, lambda *_: (0,) * len(shape), pipeline_mode=pl.Buffered(1))


def _rows(start, size):
    return pl.ds(ROW_PITCH * start, size, stride=ROW_PITCH)


def _mix_kernel(x_ref, w_in_ref, b_in_ref, wpool_ref, pscale_ref, sgu_g_ref, sgu_b_ref, ws_ref, bs_ref,
                cw_ref, cb_ref, cg_ref, cbeta_ref, w_out_ref, b_out_ref, g1_ref, beta1_ref,
                wf1_ref, wf2_ref,
                o_ref, wf1b_ref, wf2b_ref,
                proj, mixed, ybuf, vbuf, pbuf, hbuf, cbuf):
    ts = SEQ_TILE
    i = pl.program_id(1)

    wf1b_ref[...] = wf1_ref[...].astype(BF16)
    wf2b_ref[...] = wf2_ref[...].astype(BF16)

    @pl.when(i == 0)
    def _():
        for g in range(len(POOL_WINDOWS)):
            pbuf[g, _rows(0, POOL_HALO), :] = jnp.zeros((POOL_HALO, LANES), F32)
        for c in range(CONV_WIDTH // LANES):
            hbuf[c, _rows(0, CONV_HALO), :] = jnp.zeros((CONV_HALO, LANES), F32)

    xb = x_ref[...].astype(BF16)

    def in_proj(lo, hi):
        proj[:, lo:hi] = (jnp.dot(xb, w_in_ref[:, lo:hi], preferred_element_type=F32)
                          + b_in_ref[:, lo:hi])

    for c in range(CONV_WIDTH // LANES):
        in_proj(CONV_A_COL + 2 * c * LANES, CONV_A_COL + 2 * (c + 1) * LANES)
    in_proj(SGU_U_COL, CONV_A_COL)
    in_proj(0, POOL_WIDTH)

    tap0 = CONV_HALO - (CONV_KERNEL - 1)
    for c in range(CONV_WIDTH // LANES):
        cs = slice(c * LANES, (c + 1) * LANES)
        a = proj[:, CONV_A_COL + 2 * c * LANES:CONV_A_COL + (2 * c + 1) * LANES]
        gate = proj[:, CONV_A_COL + (2 * c + 1) * LANES:CONV_A_COL + (2 * c + 2) * LANES]
        hbuf[c, _rows(CONV_HALO, ts), :] = a * jax.nn.sigmoid(gate)
        for rb in range(ts // CONV_ROWS):
            r0 = rb * CONV_ROWS
            acc = jnp.broadcast_to(cb_ref[:, cs], (CONV_ROWS, LANES))
            for k in range(CONV_KERNEL):
                acc = acc + cw_ref[k:k + 1, cs] * hbuf[c, _rows(r0 + tap0 + k, CONV_ROWS), :]
            cbuf[r0:r0 + CONV_ROWS, cs] = acc
        hbuf[c, _rows(0, CONV_HALO), :] = hbuf[c, _rows(ts, CONV_HALO), :]
    for r0 in range(0, ts, ROW_BLOCK):
        rs = slice(r0, r0 + ROW_BLOCK)
        y = _layer_norm(cbuf[rs, :], cg_ref[...], cbeta_ref[...])
        mixed[rs, CONV_OUT_COL:CONV_OUT_COL + CONV_WIDTH] = jax.nn.silu(y).astype(BF16)

    tri_r = lax.broadcasted_iota(jnp.int32, (CHUNK, CHUNK), 0)
    tri_c = lax.broadcasted_iota(jnp.int32, (CHUNK, CHUNK), 1)
    ws = [jnp.where(tri_r >= tri_c, ws_ref[hd], 0.0).astype(BF16) for hd in range(SGU_HEADS)]
    for r0 in range(0, ts, ROW_BLOCK):
        rs = slice(r0, r0 + ROW_BLOCK)
        v = jax.nn.gelu(proj[rs, SGU_V_COL:SGU_V_COL + SGU_WIDTH])
        vbuf[rs, :] = _layer_norm(v, sgu_g_ref[...], sgu_b_ref[...]).astype(BF16)
    for n in range(ts // CHUNK):
        rs = slice(n * CHUNK, (n + 1) * CHUNK)
        for hd in range(SGU_HEADS):
            hs = slice(hd * HEAD_DIM, (hd + 1) * HEAD_DIM)
            m = jnp.dot(ws[hd], vbuf[rs, hs], preferred_element_type=F32) + bs_ref[:, hs]
            u = jax.nn.gelu(proj[rs, SGU_U_COL + hd * HEAD_DIM:SGU_U_COL + (hd + 1) * HEAD_DIM])
            mixed[rs, SGU_OUT_COL + hd * HEAD_DIM:SGU_OUT_COL + (hd + 1) * HEAD_DIM] = (
                u * m).astype(BF16)

    row = i * ts + lax.broadcasted_iota(jnp.int32, (ts, LANES), 0)
    for g, win in enumerate(POOL_WINDOWS):
        cs = slice(g * POOL_GROUP, (g + 1) * POOL_GROUP)
        cur = proj[:, cs]
        pbuf[g, _rows(POOL_HALO, ts), :] = cur
        wsum = cur
        for k in range(1, win):
            wsum = wsum + pbuf[g, _rows(POOL_HALO - k, ts), :]
        pbuf[g, _rows(0, POOL_HALO), :] = pbuf[g, _rows(ts, POOL_HALO), :]
        cnt = jnp.minimum(row + 1, win).astype(F32)
        pooled = wsum / cnt - cur
        y = jnp.dot(pooled.astype(BF16), wpool_ref[g], preferred_element_type=F32)
        mixed[:, cs] = (y * pscale_ref[:, cs]).astype(BF16)

    ybuf[...] = jnp.dot(mixed[...], w_out_ref[...], preferred_element_type=F32)
    for r0 in range(0, ts, LN_ROWS):
        rs = slice(r0, r0 + LN_ROWS)
        z = DEEPNORM_ALPHA * x_ref[rs, :] + (ybuf[rs, :] + b_out_ref[...])
        o_ref[rs, :] = _layer_norm(z, g1_ref[...], beta1_ref[...])


def _mix(layer, x, w_in, b_in, wpool, pscale, sgu_g, sgu_b, ws, bs_full, cw, cb, cg, cbeta,
         w_out, b_out, g1, beta1, w_ff1, w_ff2):
    bsz, s, _ = x.shape
    nt = s // SEQ_TILE
    steps = bsz * nt
    f1_rows = D_MODEL // steps
    f2_rows = D_FF // steps
    assert f1_rows * steps == D_MODEL and f1_rows % BF16_ROWS == 0
    step = lambda b, i: b * nt + i
    wres = lambda b, i: (layer, 0, 0)
    return pl.pallas_call(
        _mix_kernel,
        grid=(bsz, nt),
        in_specs=[
            pl.BlockSpec((None, SEQ_TILE, D_MODEL), lambda b, i: (b, i, 0)),
            pl.BlockSpec((None, D_MODEL, IN_WIDTH), wres, pipeline_mode=pl.Buffered(1)),
            _resident((1, IN_WIDTH)),
            _resident((len(POOL_WINDOWS), POOL_GROUP, POOL_GROUP)),
            _resident((1, POOL_WIDTH)),
            _resident((1, SGU_WIDTH)),
            _resident((1, SGU_WIDTH)),
            _resident((SGU_HEADS, CHUNK, CHUNK)),
            _resident((CHUNK, SGU_WIDTH)),
            _resident((CONV_KERNEL, CONV_WIDTH)),
            _resident((1, CONV_WIDTH)),
            _resident((1, CONV_WIDTH)),
            _resident((1, CONV_WIDTH)),
            pl.BlockSpec((None, D_MODEL, D_MODEL), wres, pipeline_mode=pl.Buffered(1)),
            _resident((1, D_MODEL)),
            _resident((1, D_MODEL)),
            _resident((1, D_MODEL)),
            pl.BlockSpec((None, f1_rows, D_FF), lambda b, i: (layer, step(b, i), 0)),
            pl.BlockSpec((None, f2_rows, D_MODEL), lambda b, i: (layer, step(b, i), 0)),
        ],
        out_specs=[
            pl.BlockSpec((None, SEQ_TILE, D_MODEL), lambda b, i: (b, i, 0)),
            pl.BlockSpec((f1_rows, D_FF), lambda b, i: (step(b, i), 0)),
            pl.BlockSpec((f2_rows, D_MODEL), lambda b, i: (step(b, i), 0)),
        ],
        out_shape=[
            jax.ShapeDtypeStruct((bsz, s, D_MODEL), F32),
            jax.ShapeDtypeStruct((D_MODEL, D_FF), BF16),
            jax.ShapeDtypeStruct((D_FF, D_MODEL), BF16),
        ],
        scratch_shapes=[
            pltpu.VMEM((SEQ_TILE, IN_WIDTH), F32),
            pltpu.VMEM((SEQ_TILE, D_MODEL), BF16),
            pltpu.VMEM((SEQ_TILE, D_MODEL), F32),
            pltpu.VMEM((SEQ_TILE, SGU_WIDTH), BF16),
            pltpu.VMEM((len(POOL_WINDOWS), ROW_PITCH * (POOL_HALO + SEQ_TILE), LANES), F32),
            pltpu.VMEM((CONV_WIDTH // LANES, ROW_PITCH * (CONV_HALO + SEQ_TILE), LANES), F32),
            pltpu.VMEM((SEQ_TILE, CONV_WIDTH), F32),
        ],
        compiler_params=pltpu.CompilerParams(
            dimension_semantics=("arbitrary", "arbitrary"), vmem_limit_bytes=VMEM_LIMIT_BYTES),
        name="mix",
    )(x, w_in, b_in, wpool, pscale, sgu_g, sgu_b, ws, bs_full, cw, cb, cg, cbeta,
      w_out, b_out, g1, beta1, w_ff1, w_ff2)


def _ffn_kernel(x_ref, w1_ref, b1_ref, w2_ref, b2_ref, g_ref, beta_ref, o_ref, xb_ref):
    k = pl.program_id(1)

    @pl.when(k == 0)
    def _():
        xb_ref[...] = x_ref[...].astype(BF16)
        o_ref[...] = jnp.zeros_like(o_ref)

    h = jnp.dot(xb_ref[...], w1_ref[...], preferred_element_type=F32) + b1_ref[pl.ds(k, 1), :]
    h = jnp.square(jnp.maximum(h, 0.0)).astype(BF16)
    o_ref[...] += jnp.dot(h, w2_ref[...], preferred_element_type=F32)

    @pl.when(k == pl.num_programs(1) - 1)
    def _():
        z = DEEPNORM_ALPHA * x_ref[...] + (o_ref[...] + b2_ref[...])
        o_ref[...] = _layer_norm(z, g_ref[...], beta_ref[...])


def _ffn(x, w1, b1, w2, b2, g, beta):
    m = x.shape[0]
    return pl.pallas_call(
        _ffn_kernel,
        grid=(m // FFN_ROW_TILE, D_FF // FFN_COL_TILE),
        in_specs=[
            pl.BlockSpec((FFN_ROW_TILE, D_MODEL), lambda i, k: (i, 0)),
            pl.BlockSpec((D_MODEL, FFN_COL_TILE), lambda i, k: (0, k)),
            _resident((D_FF // FFN_COL_TILE, FFN_COL_TILE)),
            pl.BlockSpec((FFN_COL_TILE, D_MODEL), lambda i, k: (k, 0)),
            _resident((1, D_MODEL)),
            _resident((1, D_MODEL)),
            _resident((1, D_MODEL)),
        ],
        out_specs=pl.BlockSpec((FFN_ROW_TILE, D_MODEL), lambda i, k: (i, 0)),
        out_shape=jax.ShapeDtypeStruct((m, D_MODEL), F32),
        scratch_shapes=[pltpu.VMEM((FFN_ROW_TILE, D_MODEL), BF16)],
        compiler_params=pltpu.CompilerParams(
            dimension_semantics=("arbitrary", "arbitrary"), vmem_limit_bytes=VMEM_LIMIT_BYTES),
        name="ffn",
    )(x, w1, b1, w2, b2, g, beta)


def _interleave_conv_cols(p):
    lead = p.shape[:-1]
    conv = p[..., CONV_A_COL:].reshape(*lead, 2, CONV_WIDTH // LANES, LANES)
    conv = jnp.swapaxes(conv, -3, -2).reshape(*lead, 2 * CONV_WIDTH)
    return jnp.concatenate([p[..., :CONV_A_COL], conv], axis=-1)


def kernel(x, w_in, b_in, w_pool, pool_scale, sgu_ln_g, sgu_ln_b, sgu_w, sgu_b, conv_w, conv_b,
           conv_ln_g, conv_ln_b, w_out, b_out, ln1_g, ln1_b, w_ff1, b_ff1, w_ff2, b_ff2, ln2_g, ln2_b):
    bsz, s, d = x.shape
    assert d == D_MODEL and s % SEQ_TILE == 0 and (bsz * s) % FFN_ROW_TILE == 0
    row = lambda p: p.reshape(1, -1)
    w_in_b = _interleave_conv_cols(w_in.astype(BF16))
    b_in = _interleave_conv_cols(b_in)
    w_out_b = w_out.astype(BF16)
    for l in range(DEPTH):
        bs_full = jnp.repeat(sgu_b[l].T, HEAD_DIM, axis=1)
        x1, w1b, w2b = _mix(
            l, x, w_in_b, row(b_in[l]), w_pool[l].astype(BF16), row(pool_scale[l]),
            row(sgu_ln_g[l]), row(sgu_ln_b[l]), sgu_w[l], bs_full,
            conv_w[l], row(conv_b[l]), row(conv_ln_g[l]), row(conv_ln_b[l]),
            w_out_b, row(b_out[l]), row(ln1_g[l]), row(ln1_b[l]), w_ff1, w_ff2)
        x = _ffn(x1.reshape(bsz * s, d), w1b, b_ff1[l].reshape(-1, FFN_COL_TILE), w2b, row(b_ff2[l]),
                 row(ln2_g[l]), row(ln2_b[l])).reshape(bsz, s, d)
    return x
```

```python
import jax
import jax.numpy as jnp
from jax import lax
from jax.experimental import pallas as pl
from jax.experimental.pallas import tpu as pltpu

D_MODEL = 2048
DEPTH = 2
HEAD_DIM = 128
POOL_WINDOWS = (2, 4, 8, 16)
POOL_WIDTH = D_MODEL // 4
POOL_GROUP = POOL_WIDTH // len(POOL_WINDOWS)
SGU_WIDTH = (D_MODEL - POOL_WIDTH) // 2
SGU_HEADS = SGU_WIDTH // HEAD_DIM
CHUNK = 128
CONV_WIDTH = D_MODEL - POOL_WIDTH - SGU_WIDTH
CONV_KERNEL = 31
IN_WIDTH = POOL_WIDTH + 2 * SGU_WIDTH + 2 * CONV_WIDTH
D_FF = 4 * D_MODEL
DEEPNORM_ALPHA = (2 * DEPTH) ** 0.25
LN_EPS = 1e-5

SGU_U_COL = POOL_WIDTH
SGU_V_COL = POOL_WIDTH + SGU_WIDTH
CONV_A_COL = POOL_WIDTH + 2 * SGU_WIDTH
CONV_G_COL = CONV_A_COL + CONV_WIDTH
SGU_OUT_COL = POOL_WIDTH
CONV_OUT_COL = POOL_WIDTH + SGU_WIDTH

LANES = 128
SUBLANES = 8
BF16_ROWS = 16
VMEM_LIMIT_BYTES = 56 * 1024 * 1024

SEQ_TILE = 256
POOL_HALO = 16
CONV_HALO = 32
CONV_SLAB = 256
CONV_ROWS = 64
ROW_BLOCK = 32
LN_ROWS = 16
ROW_PITCH = 2
FFN_ROW_TILE = 1024
FFN_COL_TILE = 512
FFN_EDGE_CHUNK = 512

BF16 = jnp.bfloat16
F32 = jnp.float32


def _layer_norm(x, g, b):
    mu = jnp.mean(x, axis=-1, keepdims=True)
    xc = x - mu
    var = jnp.mean(xc * xc, axis=-1, keepdims=True)
    return xc * lax.rsqrt(var + LN_EPS) * g + b


def _resident(shape):
    return pl.BlockSpec(shape, lambda *_: (0,) * len(shape), pipeline_mode=pl.Buffered(1))


def _rows(start, size):
    return pl.ds(ROW_PITCH * start, size, stride=ROW_PITCH)


def _mix_kernel(x_ref, w_in_ref, b_in_ref, wpool_ref, pscale_ref, sgu_g_ref, sgu_b_ref, ws_ref, bs_ref,
                cw_ref, cb_ref, cg_ref, cbeta_ref, w_out_ref, b_out_ref, g1_ref, beta1_ref,
                wf1_ref, wf2_ref,
                o_ref, wf1b_ref, wf2b_ref,
                proj, mixed, ybuf, vbuf, pbuf, hbuf, cbuf):
    ts = SEQ_TILE
    i = pl.program_id(1)

    wf1b_ref[...] = wf1_ref[...].astype(BF16)
    wf2b_ref[...] = wf2_ref[...].astype(BF16)

    @pl.when(i == 0)
    def _():
        for g in range(len(POOL_WINDOWS)):
            pbuf[g, _rows(0, POOL_HALO), :] = jnp.zeros((POOL_HALO, LANES), F32)
        for c in range(CONV_WIDTH // LANES):
            hbuf[c, _rows(0, CONV_HALO), :] = jnp.zeros((CONV_HALO, LANES), F32)

    xb = x_ref[...].astype(BF16)

    def in_proj(lo, hi):
        proj[:, lo:hi] = (jnp.dot(xb, w_in_ref[:, lo:hi], preferred_element_type=F32)
                          + b_in_ref[:, lo:hi])

    for lo in range(0, CONV_WIDTH, CONV_SLAB):
        in_proj(CONV_A_COL + lo, CONV_A_COL + lo + CONV_SLAB)
        in_proj(CONV_G_COL + lo, CONV_G_COL + lo + CONV_SLAB)
    in_proj(SGU_U_COL, CONV_A_COL)
    in_proj(0, POOL_WIDTH)

    tap0 = CONV_HALO - (CONV_KERNEL - 1)
    for c in range(CONV_WIDTH // LANES):
        cs = slice(c * LANES, (c + 1) * LANES)
        a = proj[:, CONV_A_COL + c * LANES:CONV_A_COL + (c + 1) * LANES]
        gate = proj[:, CONV_G_COL + c * LANES:CONV_G_COL + (c + 1) * LANES]
        hbuf[c, _rows(CONV_HALO, ts), :] = a * jax.nn.sigmoid(gate)
        for rb in range(ts // CONV_ROWS):
            r0 = rb * CONV_ROWS
            acc = jnp.broadcast_to(cb_ref[:, cs], (CONV_ROWS, LANES))
            for k in range(CONV_KERNEL):
                acc = acc + cw_ref[k:k + 1, cs] * hbuf[c, _rows(r0 + tap0 + k, CONV_ROWS), :]
            cbuf[r0:r0 + CONV_ROWS, cs] = acc
        hbuf[c, _rows(0, CONV_HALO), :] = hbuf[c, _rows(ts, CONV_HALO), :]
    for r0 in range(0, ts, ROW_BLOCK):
        rs = slice(r0, r0 + ROW_BLOCK)
        y = _layer_norm(cbuf[rs, :], cg_ref[...], cbeta_ref[...])
        mixed[rs, CONV_OUT_COL:CONV_OUT_COL + CONV_WIDTH] = jax.nn.silu(y).astype(BF16)

    tri_r = lax.broadcasted_iota(jnp.int32, (CHUNK, CHUNK), 0)
    tri_c = lax.broadcasted_iota(jnp.int32, (CHUNK, CHUNK), 1)
    ws = [jnp.where(tri_r >= tri_c, ws_ref[hd], 0.0).astype(BF16) for hd in range(SGU_HEADS)]
    for r0 in range(0, ts, ROW_BLOCK):
        rs = slice(r0, r0 + ROW_BLOCK)
        v = jax.nn.gelu(proj[rs, SGU_V_COL:SGU_V_COL + SGU_WIDTH])
        vbuf[rs, :] = _layer_norm(v, sgu_g_ref[...], sgu_b_ref[...]).astype(BF16)
    for n in range(ts // CHUNK):
        rs = slice(n * CHUNK, (n + 1) * CHUNK)
        for hd in range(SGU_HEADS):
            hs = slice(hd * HEAD_DIM, (hd + 1) * HEAD_DIM)
            m = jnp.dot(ws[hd], vbuf[rs, hs], preferred_element_type=F32) + bs_ref[:, hs]
            u = jax.nn.gelu(proj[rs, SGU_U_COL + hd * HEAD_DIM:SGU_U_COL + (hd + 1) * HEAD_DIM])
            mixed[rs, SGU_OUT_COL + hd * HEAD_DIM:SGU_OUT_COL + (hd + 1) * HEAD_DIM] = (
                u * m).astype(BF16)

    row = i * ts + lax.broadcasted_iota(jnp.int32, (ts, LANES), 0)
    for g, win in enumerate(POOL_WINDOWS):
        cs = slice(g * POOL_GROUP, (g + 1) * POOL_GROUP)
        cur = proj[:, cs]
        pbuf[g, _rows(POOL_HALO, ts), :] = cur
        wsum = cur
        for k in range(1, win):
            wsum = wsum + pbuf[g, _rows(POOL_HALO - k, ts), :]
        pbuf[g, _rows(0, POOL_HALO), :] = pbuf[g, _rows(ts, POOL_HALO), :]
        cnt = jnp.minimum(row + 1, win).astype(F32)
        pooled = wsum / cnt - cur
        y = jnp.dot(pooled.astype(BF16), wpool_ref[g], preferred_element_type=F32)
        mixed[:, cs] = (y * pscale_ref[:, cs]).astype(BF16)

    ybuf[...] = jnp.dot(mixed[...], w_out_ref[...], preferred_element_type=F32)
    for r0 in range(0, ts, LN_ROWS):
        rs = slice(r0, r0 + LN_ROWS)
        z = DEEPNORM_ALPHA * x_ref[rs, :] + (ybuf[rs, :] + b_out_ref[...])
        o_ref[rs, :] = _layer_norm(z, g1_ref[...], beta1_ref[...])


def _mix(layer, x, w_in, b_in, wpool, pscale, sgu_g, sgu_b, ws, bs_full, cw, cb, cg, cbeta,
         w_out, b_out, g1, beta1, w_ff1, w_ff2):
    bsz, s, _ = x.shape
    nt = s // SEQ_TILE
    steps = bsz * nt
    f1_rows = D_MODEL // steps
    f2_rows = D_FF // steps
    assert f1_rows * steps == D_MODEL and f1_rows % BF16_ROWS == 0
    step = lambda b, i: b * nt + i
    wres = lambda b, i: (layer, 0, 0)
    return pl.pallas_call(
        _mix_kernel,
        grid=(bsz, nt),
        in_specs=[
            pl.BlockSpec((None, SEQ_TILE, D_MODEL), lambda b, i: (b, i, 0)),
            pl.BlockSpec((None, D_MODEL, IN_WIDTH), wres, pipeline_mode=pl.Buffered(1)),
            _resident((1, IN_WIDTH)),
            _resident((len(POOL_WINDOWS), POOL_GROUP, POOL_GROUP)),
            _resident((1, POOL_WIDTH)),
            _resident((1, SGU_WIDTH)),
            _resident((1, SGU_WIDTH)),
            _resident((SGU_HEADS, CHUNK, CHUNK)),
            _resident((CHUNK, SGU_WIDTH)),
            _resident((CONV_KERNEL, CONV_WIDTH)),
            _resident((1, CONV_WIDTH)),
            _resident((1, CONV_WIDTH)),
            _resident((1, CONV_WIDTH)),
            pl.BlockSpec((None, D_MODEL, D_MODEL), wres, pipeline_mode=pl.Buffered(1)),
            _resident((1, D_MODEL)),
            _resident((1, D_MODEL)),
            _resident((1, D_MODEL)),
            pl.BlockSpec((None, f1_rows, D_FF), lambda b, i: (layer, step(b, i), 0)),
            pl.BlockSpec((None, f2_rows, D_MODEL), lambda b, i: (layer, step(b, i), 0)),
        ],
        out_specs=[
            pl.BlockSpec((None, SEQ_TILE, D_MODEL), lambda b, i: (b, i, 0)),
            pl.BlockSpec((f1_rows, D_FF), lambda b, i: (step(b, i), 0)),
            pl.BlockSpec((f2_rows, D_MODEL), lambda b, i: (step(b, i), 0)),
        ],
        out_shape=[
            jax.ShapeDtypeStruct((bsz, s, D_MODEL), F32),
            jax.ShapeDtypeStruct((D_MODEL, D_FF), BF16),
            jax.ShapeDtypeStruct((D_FF, D_MODEL), BF16),
        ],
        scratch_shapes=[
            pltpu.VMEM((SEQ_TILE, IN_WIDTH), F32),
            pltpu.VMEM((SEQ_TILE, D_MODEL), BF16),
            pltpu.VMEM((SEQ_TILE, D_MODEL), F32),
            pltpu.VMEM((SEQ_TILE, SGU_WIDTH), BF16),
            pltpu.VMEM((len(POOL_WINDOWS), ROW_PITCH * (POOL_HALO + SEQ_TILE), LANES), F32),
            pltpu.VMEM((CONV_WIDTH // LANES, ROW_PITCH * (CONV_HALO + SEQ_TILE), LANES), F32),
            pltpu.VMEM((SEQ_TILE, CONV_WIDTH), F32),
        ],
        compiler_params=pltpu.CompilerParams(
            dimension_semantics=("arbitrary", "arbitrary"), vmem_limit_bytes=VMEM_LIMIT_BYTES),
        name="mix",
    )(x, w_in, b_in, wpool, pscale, sgu_g, sgu_b, ws, bs_full, cw, cb, cg, cbeta,
      w_out, b_out, g1, beta1, w_ff1, w_ff2)


def _ffn_kernel(x_ref, w1_ref, b1_ref, w2_ref, b2_ref, g_ref, beta_ref, o_ref, xb_ref):
    k = pl.program_id(1)
    last_k = pl.num_programs(1) - 1

    def step(first, last, chunk):
        for r0 in range(0, FFN_ROW_TILE, chunk):
            rs = slice(r0, r0 + chunk)
            if first:
                xb_ref[rs, :] = x_ref[rs, :].astype(BF16)
            h = (jnp.dot(xb_ref[rs, :], w1_ref[...], preferred_element_type=F32)
                 + b1_ref[pl.ds(k, 1), :])
            h = jnp.square(jnp.maximum(h, 0.0)).astype(BF16)
            acc = jnp.dot(h, w2_ref[...], preferred_element_type=F32)
            if not first:
                acc = o_ref[rs, :] + acc
            if last:
                z = DEEPNORM_ALPHA * x_ref[rs, :] + (acc + b2_ref[...])
                acc = _layer_norm(z, g_ref[...], beta_ref[...])
            o_ref[rs, :] = acc

    pl.when(k == 0)(lambda: step(True, False, FFN_EDGE_CHUNK))
    pl.when(jnp.logical_and(k > 0, k < last_k))(lambda: step(False, False, FFN_ROW_TILE))
    pl.when(k == last_k)(lambda: step(False, True, FFN_EDGE_CHUNK))


def _ffn(x, w1, b1, w2, b2, g, beta):
    m = x.shape[0]
    return pl.pallas_call(
        _ffn_kernel,
        grid=(m // FFN_ROW_TILE, D_FF // FFN_COL_TILE),
        in_specs=[
            pl.BlockSpec((FFN_ROW_TILE, D_MODEL), lambda i, k: (i, 0)),
            pl.BlockSpec((D_MODEL, FFN_COL_TILE), lambda i, k: (0, k)),
            _resident((D_FF // FFN_COL_TILE, FFN_COL_TILE)),
            pl.BlockSpec((FFN_COL_TILE, D_MODEL), lambda i, k: (k, 0)),
            _resident((1, D_MODEL)),
            _resident((1, D_MODEL)),
            _resident((1, D_MODEL)),
        ],
        out_specs=pl.BlockSpec((FFN_ROW_TILE, D_MODEL), lambda i, k: (i, 0)),
        out_shape=jax.ShapeDtypeStruct((m, D_MODEL), F32),
        scratch_shapes=[pltpu.VMEM((FFN_ROW_TILE, D_MODEL), BF16)],
        compiler_params=pltpu.CompilerParams(
            dimension_semantics=("arbitrary", "arbitrary"), vmem_limit_bytes=VMEM_LIMIT_BYTES),
        name="ffn",
    )(x, w1, b1, w2, b2, g, beta)


def kernel(x, w_in, b_in, w_pool, pool_scale, sgu_ln_g, sgu_ln_b, sgu_w, sgu_b, conv_w, conv_b,
           conv_ln_g, conv_ln_b, w_out, b_out, ln1_g, ln1_b, w_ff1, b_ff1, w_ff2, b_ff2, ln2_g, ln2_b):
    bsz, s, d = x.shape
    assert d == D_MODEL and s % SEQ_TILE == 0 and (bsz * s) % FFN_ROW_TILE == 0
    row = lambda p: p.reshape(1, -1)
    w_in_b = w_in.astype(BF16)
    w_out_b = w_out.astype(BF16)
    for l in range(DEPTH):
        bs_full = jnp.repeat(sgu_b[l].T, HEAD_DIM, axis=1)
        x1, w1b, w2b = _mix(
            l, x, w_in_b, row(b_in[l]), w_pool[l].astype(BF16), row(pool_scale[l]),
            row(sgu_ln_g[l]), row(sgu_ln_b[l]), sgu_w[l], bs_full,
            conv_w[l], row(conv_b[l]), row(conv_ln_g[l]), row(conv_ln_b[l]),
            w_out_b, row(b_out[l]), row(ln1_g[l]), row(ln1_b[l]), w_ff1, w_ff2)
        x = _ffn(x1.reshape(bsz * s, d), w1b, b_ff1[l].reshape(-1, FFN_COL_TILE), w2b, row(b_ff2[l]),
                 row(ln2_g[l]), row(ln2_b[l])).reshape(bsz, s, d)
    return x
```
